```python
import math
import jax, jax.numpy as jnp
from jax import lax
import numpy as np

D_MODEL = 2048
BATCH = 16
SEQ = 256
DEPTH = 4
DEC_BATCH = 2
DEC_SEQ = 4096
PAST_LEN = 256

F32 = jnp.float32
GRID_W = 64
N_AB_LAYERS = (DEPTH + 1) // 2
N_C_LAYERS = DEPTH // 2
MIX_WIDTH = 2 * D_MODEL
SSD_INNER = MIX_WIDTH // 2
SSD_HEAD_DIM = 64
SSD_HEADS = SSD_INNER // SSD_HEAD_DIM
SSD_GROUPS = 4
SSD_HEADS_PER_GROUP = SSD_HEADS // SSD_GROUPS
D_STATE = 128
SSD_CONV = 3
SSD_CONV_DIM = SSD_INNER + 2 * SSD_GROUPS * D_STATE
RET_INNER = MIX_WIDTH // 2
RET_HEADS = 8
RET_DK = RET_INNER // RET_HEADS
RET_DV = RET_INNER // RET_HEADS
AB_IN_DIM = SSD_INNER + SSD_CONV_DIM + 2 * SSD_HEADS + 4 * RET_INNER
CHUNK = 128
ROPE_BASE = 10000.0
HY_WIDTH = D_MODEL
HY_ORDER = 2
HY_SHORT = 3
HY_BANDS = 16
HY_EMB = 1 + 2 * HY_BANDS
HY_HIDDEN = 64
HY_SIN_FREQ = 1.0
HY_FAST_DECAY = 0.3
HY_SLOW_DECAY = 1.5
HY_TARGET = 1e-2
D_FF = 5632
FFN_CONV = 3
EPS = 1e-6

kernel_name = 'bidir_ssd_retention_hyena_prefix_dit_step'


def rms_norm(x, w):
    xf = x.astype(F32)
    y = xf * lax.rsqrt(jnp.mean(xf * xf, axis=-1, keepdims=True) + EPS)
    return (y * w.astype(F32)).astype(x.dtype)


def group_rms_norm(x, w, groups):
    shp = x.shape
    xf = x.astype(F32).reshape(*shp[:-1], groups, shp[-1] // groups)
    xf = xf * lax.rsqrt(jnp.mean(xf * xf, axis=-1, keepdims=True) + EPS)
    return xf.reshape(shp) * w.astype(F32)


def modulate(x, shift, scale):
    return x * (1 + scale) + shift


def dwconv(x, w, b):
    k, ch = w.shape
    y = lax.conv_general_dilated(x, w[:, None, :].astype(x.dtype), window_strides=(1,),
                                 padding=[(k // 2, k // 2)], dimension_numbers=('NWC', 'WIO', 'NWC'),
                                 feature_group_count=ch)
    return y + b.astype(x.dtype)


def rope_half(t, pos):
    n = t.shape[-1] // 2
    inv = ROPE_BASE ** (-jnp.arange(n, dtype=F32) / n)
    ang = pos[:, None] * inv[None, :]
    cos = jnp.cos(ang)[None, :, None, :]
    sin = jnp.sin(ang)[None, :, None, :]
    t1 = t[..., :n].astype(F32)
    t2 = t[..., n:].astype(F32)
    return jnp.concatenate([t1 * cos - t2 * sin, t1 * sin + t2 * cos], axis=-1)


def axial_rope(x, rows, cols):
    half = x.shape[-1] // 2
    return jnp.concatenate([rope_half(x[..., :half], rows), rope_half(x[..., half:], cols)], axis=-1).astype(x.dtype)


def chunked_decay_scan(q, k, v, log_a, h0):
    b, L, G, N = q.shape
    R, P = v.shape[3], v.shape[4]
    nc = L // CHUNK
    qc = q.reshape(b, nc, CHUNK, G, N)
    kc = k.reshape(b, nc, CHUNK, G, N)
    vc = v.reshape(b, nc, CHUNK, G, R, P)
    a_cum = jnp.cumsum(log_a.astype(F32).reshape(b, nc, CHUNK, G, R), axis=2)
    idx = jnp.arange(CHUNK)
    lower = (idx[:, None] >= idx[None, :])[None, None, :, :, None, None]
    seg = a_cum[:, :, :, None] - a_cum[:, :, None, :]
    decay = jnp.exp(jnp.where(lower, seg, -jnp.inf))
    scores = jnp.einsum('bcign,bcjgn->bcijg', qc, kc)
    y_intra = jnp.einsum('bcijg,bcijgr,bcjgrp->bcigrp', scores, decay, vc)
    to_end = jnp.exp(a_cum[:, :, -1:] - a_cum)
    local = jnp.einsum('bcjgn,bcjgr,bcjgrp->bcgrpn', kc, to_end, vc)
    chunk_decay = jnp.exp(a_cum[:, :, -1])

    def step(h, inp):
        loc, dec = inp
        return h * dec[..., None, None] + loc, h

    final, h_in = lax.scan(step, h0.astype(F32), (jnp.moveaxis(local, 1, 0), jnp.moveaxis(chunk_decay, 1, 0)))
    h_in = jnp.moveaxis(h_in, 0, 1)
    y_inter = jnp.einsum('bcign,bcgrpn,bcigr->bcigrp', qc, h_in, jnp.exp(a_cum))
    return (y_intra + y_inter).reshape(b, L, G, R, P), final


def bidirectional_scan(q, k, v_dirs, log_a_dirs, h0):
    rev = lambda t: jnp.flip(t, axis=1)
    y_f, s_f = chunked_decay_scan(q, k, v_dirs[0], log_a_dirs[0], h0[:, 0])
    y_b, s_b = chunked_decay_scan(rev(q), rev(k), rev(v_dirs[1]), rev(log_a_dirs[1]), h0[:, 1])
    return y_f + rev(y_b), jnp.stack([s_f, s_b], axis=1)


def ab_mixer(u, pos, h0_ssd, h0_ret, w_in, conv_w, conv_b, dt_bias, a_log, d_skip, ssd_norm_w,
             log_gamma, ret_norm_w, w_out):
    b, L, _ = u.shape
    G, R, P, N = SSD_GROUPS, SSD_HEADS_PER_GROUP, SSD_HEAD_DIM, D_STATE
    o1 = SSD_INNER
    o2 = o1 + SSD_CONV_DIM
    o3 = o2 + 2 * SSD_HEADS
    o4 = o3 + RET_INNER
    o5 = o4 + RET_INNER
    o6 = o5 + RET_INNER
    proj = u @ w_in
    z, xbc, dt_raw, q, k, v, g = jnp.split(proj, [o1, o2, o3, o4, o5, o6], axis=-1)
    xbc = jax.nn.silu(dwconv(xbc, conv_w, conv_b))
    xs, bm, cm = jnp.split(xbc, [SSD_INNER, SSD_INNER + G * N], axis=-1)
    xs = xs.reshape(b, L, G, R, P)
    bm = bm.reshape(b, L, G, N)
    cm = cm.reshape(b, L, G, N)
    dt = jax.nn.softplus(dt_raw.astype(F32).reshape(b, L, 2, G, R) + dt_bias.astype(F32).reshape(2, G, R))
    a = -jnp.exp(a_log.astype(F32)).reshape(2, G, R)
    y_s, st_s = bidirectional_scan(
        cm, bm,
        (xs * dt[:, :, 0, :, :, None], xs * dt[:, :, 1, :, :, None]),
        (dt[:, :, 0] * a[0], dt[:, :, 1] * a[1]),
        h0_ssd.reshape(b, 2, G, R, P, N))
    y_s = y_s + xs * d_skip.astype(F32).reshape(G, R, 1)
    y_s = y_s.reshape(b, L, SSD_INNER) * jax.nn.silu(z)
    y_s = group_rms_norm(y_s, ssd_norm_w, SSD_GROUPS)
    q = q.reshape(b, L, RET_HEADS, RET_DK)
    k = k.reshape(b, L, RET_HEADS, RET_DK) * (RET_DK ** -0.5)
    if pos is not None:
        q = axial_rope(q, pos[0], pos[1])
        k = axial_rope(k, pos[0], pos[1])
    v = v.reshape(b, L, RET_HEADS, 1, RET_DV)
    lg = log_gamma.astype(F32)
    la = (jnp.broadcast_to(lg[0][None, None, :, None], (b, L, RET_HEADS, 1)),
          jnp.broadcast_to(lg[1][None, None, :, None], (b, L, RET_HEADS, 1)))
    y_r, st_r = bidirectional_scan(q, k, (v, v), la, h0_ret.reshape(b, 2, RET_HEADS, 1, RET_DV, RET_DK))
    y_r = y_r.reshape(b, L, RET_HEADS, RET_DV)
    mu = jnp.mean(y_r, axis=-1, keepdims=True)
    var = jnp.mean(jnp.square(y_r - mu), axis=-1, keepdims=True)
    y_r = ((y_r - mu) * lax.rsqrt(var + EPS)).reshape(b, L, RET_INNER) * ret_norm_w.astype(F32)
    y_r = jax.nn.silu(g) * y_r
    out = jnp.concatenate([y_s.astype(u.dtype), y_r.astype(u.dtype)], axis=-1) @ w_out
    return (out, st_s.reshape(b, 2, SSD_HEADS, P, N), st_r.reshape(b, 2, RET_HEADS, RET_DV, RET_DK))


def hyena_filters(L, w1, b1, w2, b2, w3):
    t = jnp.arange(L, dtype=F32) / L
    bands = jnp.linspace(1e-4, HY_BANDS - 1, HY_BANDS, dtype=F32)
    ang = 2 * math.pi * t[:, None] * bands[None, :]
    feats = jnp.concatenate([t[:, None], jnp.cos(ang), jnp.sin(ang)], axis=-1)
    h = jnp.sin(HY_SIN_FREQ * (feats @ w1.astype(F32) + b1.astype(F32)))
    h = jnp.sin(HY_SIN_FREQ * (h @ w2.astype(F32) + b2.astype(F32)))
    h = (h @ w3.astype(F32)).reshape(L, 2, HY_ORDER, HY_WIDTH)
    deltas = jnp.abs(jnp.linspace(math.log(HY_TARGET) / HY_SLOW_DECAY, math.log(HY_TARGET) / HY_FAST_DECAY,
                                  HY_WIDTH, dtype=F32))
    window = jnp.exp(-t[:, None] * deltas[None, :])
    h = h * window[:, None, None, :]
    return h / (jnp.sum(jnp.abs(h), axis=0, keepdims=True) + EPS)


def bidir_long_conv(z, h_f, h_b, bias):
    L, ch = h_f.shape
    filt = jnp.concatenate([h_f, jnp.zeros((1, ch), h_f.dtype), jnp.flip(h_b[1:], axis=0)], axis=0)
    zf = jnp.fft.rfft(z.astype(F32), n=2 * L, axis=1)
    ff = jnp.fft.rfft(filt, n=2 * L, axis=0)
    y = jnp.fft.irfft(zf * ff[None], n=2 * L, axis=1)[:, :L]
    return y + z.astype(F32) * bias.astype(F32)


def hyena_mixer(u, w_in, short_w, short_b, f_w1, f_b1, f_w2, f_b2, f_w3, bias, w_out):
    L = u.shape[1]
    proj = dwconv(u @ w_in, short_w, short_b)
    x1, x2, v = jnp.split(proj, 3, axis=-1)
    filt = hyena_filters(L, f_w1, f_b1, f_w2, f_b2, f_w3)
    z = v
    for o, gate in enumerate((x1, x2)):
        z = gate * bidir_long_conv(z, filt[:, 0, o], filt[:, 1, o], bias[o])
    return z.astype(u.dtype) @ w_out


def conv_glu(u, w_gate, w_up, conv_w, conv_b, w_down):
    a = dwconv(u @ w_gate, conv_w, conv_b)
    return (jax.nn.gelu(a) * (u @ w_up)) @ w_down


def trunk(x, cond, pos, h0_ssd, h0_ret, p):
    silu_c = jax.nn.silu(cond.astype(F32))
    fin_ssd, fin_ret = [], []
    for l in range(DEPTH):
        mod = silu_c @ p['ada_w'][l].astype(F32) + p['ada_b'][l].astype(F32)
        sh1, sc1, g1, sh2, sc2, g2 = jnp.split(mod.astype(x.dtype)[:, None, :], 6, axis=-1)
        h = modulate(rms_norm(x, p['norm1_w'][l]), sh1, sc1)
        i = l // 2
        if l % 2 == 0:
            mix, st_s, st_r = ab_mixer(h, pos, h0_ssd[:, i], h0_ret[:, i], p['ab_w_in'][i], p['ab_conv_w'][i],
                                       p['ab_conv_b'][i], p['ssd_dt_bias'][i], p['ssd_a_log'][i], p['ssd_d'][i],
                                       p['ssd_norm_w'][i], p['ret_log_gamma'][i], p['ret_norm_w'][i],
                                       p['ab_w_out'][i])
            fin_ssd.append(st_s)
            fin_ret.append(st_r)
        else:
            mix = hyena_mixer(h, p['hy_w_in'][i], p['hy_short_w'][i], p['hy_short_b'][i], p['hy_f_w1'][i],
                              p['hy_f_b1'][i], p['hy_f_w2'][i], p['hy_f_b2'][i], p['hy_f_w3'][i],
                              p['hy_bias'][i], p['hy_w_out'][i])
        x = x + (g1 * mix).astype(x.dtype)
        h = modulate(rms_norm(x, p['norm2_w'][l]), sh2, sc2)
        x = x + (g2 * conv_glu(h, p['ffn_w_gate'][l], p['ffn_w_up'][l], p['ffn_conv_w'][l], p['ffn_conv_b'][l],
                               p['ffn_w_down'][l])).astype(x.dtype)
    return rms_norm(x, p['final_norm_w']), fin_ssd, fin_ret


def setup_inputs(seed: int = 0) -> dict:
    key = jax.random.key(seed)
    keys = list(jax.random.split(key, 48))

    def nrm(shape, scale):
        return jax.random.normal(keys.pop(), shape, F32) * scale

    def gain(shape):
        return 1.0 + nrm(shape, 0.02)

    LA, LC = N_AB_LAYERS, N_C_LAYERS
    dt0 = jnp.exp(jax.random.uniform(keys.pop(), (LA, 2, SSD_HEADS), F32, math.log(1e-3), math.log(1e-1)))
    a0 = jax.random.uniform(keys.pop(), (LA, 2, SSD_HEADS), F32, 1.0, 16.0)
    gamma_base = jnp.log(1.0 - 2.0 ** (-5.0 - jnp.arange(RET_HEADS, dtype=F32)))
    hy_in = (HY_ORDER + 1) * HY_WIDTH
    return {
        'x_prompt': nrm((BATCH, SEQ, D_MODEL), 1.0),
        'x_sample': nrm((DEC_BATCH, DEC_SEQ, D_MODEL), 1.0),
        'c': nrm((DEC_BATCH, D_MODEL), 1.0),
        'state_ssd': nrm((DEC_BATCH, LA, 2, SSD_HEADS, SSD_HEAD_DIM, D_STATE), 0.1),
        'state_ret': nrm((DEC_BATCH, LA, 2, RET_HEADS, RET_DV, RET_DK), 1.0),
        'c_ctx': nrm((D_MODEL,), 1.0),
        'ada_w': nrm((DEPTH, D_MODEL, 6 * D_MODEL), 0.5 * D_MODEL ** -0.5),
        'ada_b': nrm((DEPTH, 6 * D_MODEL), 0.02),
        'norm1_w': gain((DEPTH, D_MODEL)),
        'norm2_w': gain((DEPTH, D_MODEL)),
        'ab_w_in': nrm((LA, D_MODEL, AB_IN_DIM), D_MODEL ** -0.5),
        'ab_conv_w': nrm((LA, SSD_CONV, SSD_CONV_DIM), SSD_CONV ** -0.5),
        'ab_conv_b': nrm((LA, SSD_CONV_DIM), 0.02),
        'ssd_dt_bias': dt0 + jnp.log(-jnp.expm1(-dt0)),
        'ssd_a_log': jnp.log(a0),
        'ssd_d': 1.0 + nrm((LA, SSD_HEADS), 0.1),
        'ssd_norm_w': gain((LA, SSD_INNER)),
        'ret_log_gamma': gamma_base * (1.0 + nrm((LA, 2, RET_HEADS), 0.05)),
        'ret_norm_w': gain((LA, RET_INNER)),
        'ab_w_out': nrm((LA, MIX_WIDTH, D_MODEL), MIX_WIDTH ** -0.5),
        'hy_w_in': nrm((LC, D_MODEL, hy_in), D_MODEL ** -0.5),
        'hy_short_w': nrm((LC, HY_SHORT, hy_in), HY_SHORT ** -0.5),
        'hy_short_b': nrm((LC, hy_in), 0.02),
        'hy_f_w1': nrm((LC, HY_EMB, HY_HIDDEN), HY_EMB ** -0.5),
        'hy_f_b1': nrm((LC, HY_HIDDEN), 0.02),
        'hy_f_w2': nrm((LC, HY_HIDDEN, HY_HIDDEN), HY_HIDDEN ** -0.5),
        'hy_f_b2': nrm((LC, HY_HIDDEN), 0.02),
        'hy_f_w3': nrm((LC, HY_HIDDEN, 2 * HY_ORDER * HY_WIDTH), HY_HIDDEN ** -0.5),
        'hy_bias': nrm((LC, HY_ORDER, HY_WIDTH), 1.0),
        'hy_w_out': nrm((LC, HY_WIDTH, D_MODEL), HY_WIDTH ** -0.5),
        'ffn_w_gate': nrm((DEPTH, D_MODEL, D_FF), D_MODEL ** -0.5),
        'ffn_w_up': nrm((DEPTH, D_MODEL, D_FF), D_MODEL ** -0.5),
        'ffn_conv_w': nrm((DEPTH, FFN_CONV, D_FF), FFN_CONV ** -0.5),
        'ffn_conv_b': nrm((DEPTH, D_FF), 0.02),
        'ffn_w_down': nrm((DEPTH, D_FF, D_MODEL), D_FF ** -0.5),
        'final_norm_w': gain((D_MODEL,)),
    }


def reference(x_prompt, x_sample, c, state_ssd, state_ret, c_ctx, ada_w, ada_b, norm1_w, norm2_w, ab_w_in,
              ab_conv_w, ab_conv_b, ssd_dt_bias, ssd_a_log, ssd_d, ssd_norm_w, ret_log_gamma, ret_norm_w, ab_w_out,
              hy_w_in, hy_short_w, hy_short_b, hy_f_w1, hy_f_b1, hy_f_w2, hy_f_b2, hy_f_w3, hy_bias, hy_w_out,
              ffn_w_gate, ffn_w_up, ffn_conv_w, ffn_conv_b, ffn_w_down, final_norm_w):
    p = {
        'ada_w': ada_w, 'ada_b': ada_b, 'norm1_w': norm1_w, 'norm2_w': norm2_w,
        'ab_w_in': ab_w_in, 'ab_conv_w': ab_conv_w, 'ab_conv_b': ab_conv_b, 'ssd_dt_bias': ssd_dt_bias,
        'ssd_a_log': ssd_a_log, 'ssd_d': ssd_d, 'ssd_norm_w': ssd_norm_w, 'ret_log_gamma': ret_log_gamma,
        'ret_norm_w': ret_norm_w, 'ab_w_out': ab_w_out,
        'hy_w_in': hy_w_in, 'hy_short_w': hy_short_w, 'hy_short_b': hy_short_b, 'hy_f_w1': hy_f_w1,
        'hy_f_b1': hy_f_b1, 'hy_f_w2': hy_f_w2, 'hy_f_b2': hy_f_b2, 'hy_f_w3': hy_f_w3, 'hy_bias': hy_bias,
        'hy_w_out': hy_w_out,
        'ffn_w_gate': ffn_w_gate, 'ffn_w_up': ffn_w_up, 'ffn_conv_w': ffn_conv_w, 'ffn_conv_b': ffn_conv_b,
        'ffn_w_down': ffn_w_down, 'final_norm_w': final_norm_w,
    }
    nb = x_prompt.shape[0]
    zeros_ssd = jnp.zeros((nb, N_AB_LAYERS, 2, SSD_HEADS, SSD_HEAD_DIM, D_STATE), F32)
    zeros_ret = jnp.zeros((nb, N_AB_LAYERS, 2, RET_HEADS, RET_DV, RET_DK), F32)
    y_prompt, ctx_ssd, ctx_ret = trunk(x_prompt, c_ctx[None, :], None, zeros_ssd, zeros_ret, p)
    new_state_ssd = jnp.stack(ctx_ssd, axis=1)
    new_state_ret = jnp.stack(ctx_ret, axis=1)
    lat_len = x_sample.shape[1]
    n_rows = lat_len // GRID_W
    rows = jnp.repeat(jnp.arange(n_rows, dtype=F32), GRID_W)
    cols = jnp.broadcast_to(jnp.arange(GRID_W, dtype=F32)[None, :], (n_rows, GRID_W)).reshape(-1)
    y_sample, _, _ = trunk(x_sample, c, (rows, cols), state_ssd, state_ret, p)
    return (y_prompt, y_sample, new_state_ssd, new_state_ret)
```

```python
import functools
import math

import numpy as np
import jax
import jax.numpy as jnp
from jax import lax
from jax.experimental import pallas as pl
from jax.experimental.pallas import tpu as pltpu

D_MODEL = 2048
BATCH = 16
SEQ = 256
DEPTH = 4
DEC_BATCH = 2
DEC_SEQ = 4096

F32 = jnp.float32
BF16 = jnp.bfloat16
GRID_W = 64
N_AB_LAYERS = (DEPTH + 1) // 2
N_C_LAYERS = DEPTH // 2
MIX_WIDTH = 2 * D_MODEL
SSD_INNER = MIX_WIDTH // 2
SSD_HEAD_DIM = 64
SSD_HEADS = SSD_INNER // SSD_HEAD_DIM
SSD_GROUPS = 4
SSD_HEADS_PER_GROUP = SSD_HEADS // SSD_GROUPS
D_STATE = 128
SSD_CONV = 3
SSD_CONV_DIM = SSD_INNER + 2 * SSD_GROUPS * D_STATE
RET_INNER = MIX_WIDTH // 2
RET_HEADS = 8
RET_DK = RET_INNER // RET_HEADS
RET_DV = RET_INNER // RET_HEADS
AB_IN_DIM = SSD_INNER + SSD_CONV_DIM + 2 * SSD_HEADS + 4 * RET_INNER
CHUNK = 128
ROPE_BASE = 10000.0
HY_WIDTH = D_MODEL
HY_ORDER = 2
HY_SHORT = 3
HY_BANDS = 16
HY_EMB = 1 + 2 * HY_BANDS
HY_HIDDEN = 64
HY_SIN_FREQ = 1.0
HY_FAST_DECAY = 0.3
HY_SLOW_DECAY = 1.5
HY_TARGET = 1e-2
D_FF = 5632
FFN_CONV = 3
EPS = 1e-6

T_S = DEC_BATCH * DEC_SEQ
T_C = BATCH * SEQ
T_ALL = T_S + T_C
N_COND = 8
LANES = 128
SUBLANES = 8
VMEM_CAP = 56 << 20
HIGHEST = lax.Precision.HIGHEST


def _pick(n, prefs):
    for p in prefs:
        if n % p == 0:
            return p
    raise ValueError(f"no tile for {n} in {prefs}")


def _nbytes(shape, dtype):
    return int(np.prod(shape)) * jnp.dtype(dtype).itemsize


def _params(sem, blocks, scratch=()):
    est = 2 * sum(_nbytes(s, d) for s, d in blocks) + sum(_nbytes(s, d) for s, d in scratch)
    limit = min(max(2 * est, 32 << 20), VMEM_CAP)
    return pltpu.CompilerParams(dimension_semantics=sem, vmem_limit_bytes=limit)


def _resident(shape, index_map):
    return pl.BlockSpec(shape, index_map, pipeline_mode=pl.Buffered(1))


def _cond_of_row(t0):
    return jnp.where(t0 < T_S, 1 + t0 // DEC_SEQ, 0)


def _mod_row(layer, t0, which):
    return (layer * N_COND + _cond_of_row(t0)) * 6 + which


def _bdot(a, b):
    return jnp.dot(a.astype(BF16), b.astype(BF16), preferred_element_type=F32)


def _silu(x):
    return x / (1.0 + jnp.exp(-x))


def _softplus(x):
    return jnp.maximum(x, 0.0) + jnp.log(1.0 + jnp.exp(-jnp.abs(x)))


def _gelu_tanh(x):
    return 0.5 * x * (1.0 + jnp.tanh(math.sqrt(2.0 / math.pi) * (x + 0.044715 * (x * x * x))))


def _split3(x):
    x1 = x.astype(BF16)
    r1 = x - x1.astype(F32)
    x2 = r1.astype(BF16)
    x3 = (r1 - x2.astype(F32)).astype(BF16)
    return x1, x2, x3


def _dot_exact_rhs01(x, m01):
    return sum(jnp.dot(t, m01, preferred_element_type=F32) for t in _split3(x))


def _dot_exact_lhs01(m01, x):
    return sum(jnp.dot(m01, t, preferred_element_type=F32) for t in _split3(x))


def _ada_kernel(ct_ref, w_ref, b_ref, o_ref, *, n_rows, kc):
    k_dim, tn = w_ref.shape
    s = _silu(ct_ref[...])
    outs = []
    for r in range(n_rows):
        acc = jnp.zeros((SUBLANES, tn), F32)
        for c0 in range(0, k_dim, kc):
            wv = w_ref[c0:c0 + kc, :] * s[c0:c0 + kc, r:r + 1]
            acc = acc + jnp.sum(wv.reshape(kc // SUBLANES, SUBLANES, tn), axis=0)
        outs.append(jnp.sum(acc, axis=0, keepdims=True))
    outs.append(jnp.zeros((N_COND - n_rows, tn), F32))
    o_ref[...] = jnp.concatenate(outs, axis=0) + b_ref[...]


def _ada(cond_t, ada_w, ada_b):
    depth, k_dim, n = ada_w.shape
    tn = _pick(n, (512, 256, 128))
    blocks = [((k_dim, N_COND), F32), ((k_dim, tn), F32), ((1, tn), F32), ((N_COND, tn), F32)]
    return pl.pallas_call(
        functools.partial(_ada_kernel, n_rows=1 + DEC_BATCH, kc=_pick(k_dim, (256, 128, 8))),
        out_shape=jax.ShapeDtypeStruct((depth, N_COND, n), F32),
        grid=(depth, n // tn),
        in_specs=[pl.BlockSpec((k_dim, N_COND), lambda l, j: (0, 0)),
                  pl.BlockSpec((None, k_dim, tn), lambda l, j: (l, 0, j)),
                  pl.BlockSpec((None, 1, tn), lambda l, j: (l, 0, j))],
        out_specs=pl.BlockSpec((None, N_COND, tn), lambda l, j: (l, 0, j)),
        compiler_params=_params(("arbitrary", "arbitrary"), blocks),
        name="ada_mod",
    )(cond_t, ada_w, ada_b.reshape(depth, 1, n))


def _norm_mod_kernel(x_ref, nw_ref, sh_ref, sc_ref, o_ref):
    x = x_ref[...]
    y = x * lax.rsqrt(jnp.mean(x * x, axis=-1, keepdims=True) + EPS) * nw_ref[...]
    o_ref[...] = (y * (1.0 + sc_ref[0]) + sh_ref[0]).astype(o_ref.dtype)


def _norm_mod(x, norm_w, mods, layer, shift_idx, scale_idx):
    t, d = x.shape
    tm = _pick(math.gcd(T_S, T_C), (512, 256, 128))
    blocks = [((tm, d), F32), ((tm, d), BF16), ((1, d), F32), ((1, d), F32), ((1, d), F32)]
    return pl.pallas_call(
        _norm_mod_kernel,
        out_shape=jax.ShapeDtypeStruct((t, d), BF16),
        grid=(t // tm,),
        in_specs=[pl.BlockSpec((tm, d), lambda m: (m, 0)),
                  pl.BlockSpec((None, 1, d), lambda m: (layer, 0, 0)),
                  pl.BlockSpec((1, 1, d), lambda m: (_mod_row(layer, m * tm, shift_idx), 0, 0)),
                  pl.BlockSpec((1, 1, d), lambda m: (_mod_row(layer, m * tm, scale_idx), 0, 0))],
        out_specs=pl.BlockSpec((tm, d), lambda m: (m, 0)),
        compiler_params=_params(("arbitrary",), blocks),
        name="norm_mod",
    )(x, norm_w.reshape(norm_w.shape[0], 1, d), mods, mods)


def _rms_kernel(x_ref, nw_ref, o_ref):
    x = x_ref[...]
    o_ref[...] = x * lax.rsqrt(jnp.mean(x * x, axis=-1, keepdims=True) + EPS) * nw_ref[...]


def _final_norm(x, w):
    t, d = x.shape
    tm = _pick(t, (512, 256, 128))
    blocks = [((tm, d), F32), ((tm, d), F32), ((1, d), F32)]
    return pl.pallas_call(
        _rms_kernel,
        out_shape=jax.ShapeDtypeStruct((t, d), F32),
        grid=(t // tm,),
        in_specs=[pl.BlockSpec((tm, d), lambda m: (m, 0)), pl.BlockSpec((1, d), lambda m: (0, 0))],
        out_specs=pl.BlockSpec((tm, d), lambda m: (m, 0)),
        compiler_params=_params(("arbitrary",), blocks),
        name="final_norm",
    )(x, w.reshape(1, d))


def _mm_kernel(x_ref, w_ref, o_ref, wb_ref):
    @pl.when(pl.program_id(1) == 0)
    def _():
        wb_ref[...] = w_ref[...].astype(BF16)
    o_ref[...] = jnp.dot(x_ref[...].astype(BF16), wb_ref[...], preferred_element_type=F32)


def _mm_res_kernel(x_ref, w_ref, r_ref, g_ref, o_ref, wb_ref):
    @pl.when(pl.program_id(1) == 0)
    def _():
        wb_ref[...] = w_ref[...].astype(BF16)
    acc = jnp.dot(x_ref[...].astype(BF16), wb_ref[...], preferred_element_type=F32)
    o_ref[...] = r_ref[...] + g_ref[0] * acc


def _matmul(x, w, layer, col0, ncols, name, res=None, mods=None, gate_idx=None):
    t, k_dim = x.shape
    assert w.shape[1] == k_dim
    tm = _pick(math.gcd(T_S, T_C), (512, 256, 128))
    w_tile_budget = 12 << 20
    tn = _pick(ncols, tuple(p for p in (1024, 512, 256, 128) if k_dim * p * 4 <= w_tile_budget and col0 % p == 0))
    nb0 = col0 // tn
    blocks = [((tm, k_dim), x.dtype), ((k_dim, tn), F32), ((tm, tn), F32)]
    in_specs = [pl.BlockSpec((tm, k_dim), lambda n, m: (m, 0)),
                pl.BlockSpec((None, k_dim, tn), lambda n, m: (layer, 0, n + nb0))]
    args = [x, w]
    body = _mm_kernel
    if res is not None:
        body = _mm_res_kernel
        in_specs += [pl.BlockSpec((tm, tn), lambda n, m: (m, n)),
                     pl.BlockSpec((1, 1, tn), lambda n, m: (_mod_row(gate_idx[0], m * tm, gate_idx[1]), 0, n))]
        args += [res, mods]
        blocks += [((tm, tn), F32), ((1, tn), F32)]
    return pl.pallas_call(
        body,
        out_shape=jax.ShapeDtypeStruct((t, ncols), F32),
        grid=(ncols // tn, t // tm),
        in_specs=in_specs,
        out_specs=pl.BlockSpec((tm, tn), lambda n, m: (m, n)),
        scratch_shapes=[pltpu.VMEM((k_dim, tn), BF16)],
        compiler_params=_params(("arbitrary", "arbitrary"), blocks, [((k_dim, tn), BF16)]),
        name=name,
    )(*args)


def _seq_edges(t):
    is_ctx = t >= T_S
    pos = jnp.where(is_ctx, (t - T_S) & (SEQ - 1), t & (DEC_SEQ - 1))
    last = jnp.where(is_ctx, SEQ - 1, DEC_SEQ - 1)
    return pos == 0, pos == last


def _conv3(x, xp_ref, xn_ref, w_ref, b_ref, t0):
    tm, tc = x.shape
    rows = lax.broadcasted_iota(jnp.int32, (tm, tc), 0)
    first, last = _seq_edges(t0 + rows)
    prev = pltpu.roll(x, 1, 0)
    prev = jnp.where(rows == 0, xp_ref[SUBLANES - 1:SUBLANES, :], prev)
    prev = jnp.where(first, 0.0, prev)
    nxt = pltpu.roll(x, tm - 1, 0)
    nxt = jnp.where(rows == tm - 1, xn_ref[0:1, :], nxt)
    nxt = jnp.where(last, 0.0, nxt)
    return w_ref[0:1, :] * prev + w_ref[1:2, :] * x + w_ref[2:3, :] * nxt + b_ref[...]


def _conv3_kernel(x_ref, xp_ref, xn_ref, w_ref, b_ref, o_ref, *, act):
    tm = x_ref.shape[0]
    y = _conv3(x_ref[...], xp_ref, xn_ref, w_ref, b_ref, pl.program_id(0) * tm)
    if act == "silu":
        y = _silu(y)
    o_ref[...] = y


def _conv3_glu_kernel(x_ref, xp_ref, xn_ref, w_ref, b_ref, u_ref, o_ref):
    tm = x_ref.shape[0]
    y = _conv3(x_ref[...], xp_ref, xn_ref, w_ref, b_ref, pl.program_id(0) * tm)
    o_ref[...] = (_gelu_tanh(y) * u_ref[...]).astype(o_ref.dtype)


def _dwconv(x, col0, ncols, w, b, layer, name, act=None, up=None, up_col0=0, out_dtype=F32):
    t = x.shape[0]
    tm = _pick(math.gcd(T_S, T_C), (512, 256, 128))
    tc = _pick(math.gcd(ncols, math.gcd(col0, up_col0) or ncols), (512, 256, 128))
    cb0 = col0 // tc
    hb = tm // SUBLANES
    n_hb = t // SUBLANES
    in_specs = [pl.BlockSpec((tm, tc), lambda m, j: (m, j + cb0)),
                pl.BlockSpec((SUBLANES, tc), lambda m, j: (jnp.maximum(m * hb - 1, 0), j + cb0)),
                pl.BlockSpec((SUBLANES, tc), lambda m, j: (jnp.minimum((m + 1) * hb, n_hb - 1), j + cb0)),
                pl.BlockSpec((None, w.shape[1], tc), lambda m, j: (layer, 0, j)),
                pl.BlockSpec((None, 1, tc), lambda m, j: (layer, 0, j))]
    args = [x, x, x, w, b.reshape(b.shape[0], 1, b.shape[1])]
    blocks = [((tm, tc), F32)] * 3
    if up is None:
        body = functools.partial(_conv3_kernel, act=act)
    else:
        body = _conv3_glu_kernel
        ub0 = up_col0 // tc
        in_specs.append(pl.BlockSpec((tm, tc), lambda m, j: (m, j + ub0)))
        args.append(up)
    return pl.pallas_call(
        body,
        out_shape=jax.ShapeDtypeStruct((t, ncols), out_dtype),
        grid=(t // tm, ncols // tc),
        in_specs=in_specs,
        out_specs=pl.BlockSpec((tm, tc), lambda m, j: (m, j)),
        compiler_params=_params(("arbitrary", "arbitrary"), blocks),
        name=name,
    )(*args)


assert SEQ & (SEQ - 1) == 0 and DEC_SEQ & (DEC_SEQ - 1) == 0 and SSD_HEAD_DIM & (SSD_HEAD_DIM - 1) == 0
Q = CHUNK
NCH = T_ALL // Q
NCH_S = T_S // Q
CPS_S = DEC_SEQ // Q
CPS_C = SEQ // Q


def _chunk_of(d, c):
    return jnp.where(d == 0, c, NCH - 1 - c)


def _ctx_seq_of(ce):
    return jnp.where(ce >= NCH_S, (ce - NCH_S) // CPS_C, 0)


def _dec_seq_of(ce):
    return jnp.where(ce < NCH_S, ce // CPS_S, DEC_BATCH - 1)


def _scan_flags(d, ce):
    is_ctx = ce >= NCH_S
    pos = jnp.where(is_ctx, (ce - NCH_S) % CPS_C, ce % CPS_S)
    per = jnp.where(is_ctx, CPS_C, CPS_S)
    at_lo = pos == 0
    at_hi = pos == per - 1
    first = jnp.where(d == 0, at_lo, at_hi)
    last = jnp.where(d == 0, at_hi, at_lo)
    return is_ctx, first, last


def _ssd_scan_kernel(xs_ref, b_ref, c_ref, dtr_ref, dtb_ref, alog_ref, h0_ref, y_ref, st_ref, h_ref):
    g = pl.program_id(0)
    d = pl.program_id(1)
    ce = _chunk_of(d, pl.program_id(2))
    is_ctx, first, last = _scan_flags(d, ce)
    gw = xs_ref.shape[1]
    p_dim = SSD_HEAD_DIM
    r_heads = gw // p_dim
    fwd = d == 0

    @pl.when(first & is_ctx)
    def _():
        h_ref[...] = jnp.zeros_like(h_ref)

    @pl.when(first & jnp.logical_not(is_ctx))
    def _():
        h_ref[...] = h0_ref[...].T

    dt = _softplus(dtr_ref[...] + dtb_ref[...])
    la = dt * (-jnp.exp(alog_ref[...]))
    off = d * SSD_HEADS + g * r_heads
    src = lax.broadcasted_iota(jnp.int32, (LANES, gw), 0)
    col = lax.broadcasted_iota(jnp.int32, (LANES, gw), 1)
    head_of_col = jnp.right_shift(col, p_dim.bit_length() - 1)
    expand = (src == off + head_of_col).astype(BF16)
    s2 = lax.broadcasted_iota(jnp.int32, (LANES, LANES), 0)
    c2 = lax.broadcasted_iota(jnp.int32, (LANES, LANES), 1)
    pick = ((s2 == off + c2) & (c2 < r_heads)).astype(BF16)
    expand_r = (src == head_of_col).astype(BF16)
    ii = lax.broadcasted_iota(jnp.int32, (Q, Q), 0)
    jj = lax.broadcasted_iota(jnp.int32, (Q, Q), 1)
    tri_b = (ii - jj) * jnp.where(fwd, 1, -1) >= 0
    tri = tri_b.astype(BF16)

    la_r = _dot_exact_rhs01(la, pick)
    ac_r = _dot_exact_lhs01(tri, la_r)
    acx = _dot_exact_rhs01(ac_r, expand_r)
    dtx = _dot_exact_rhs01(dt, expand)
    tot = jnp.where(fwd, acx[Q - 1:Q, :], acx[0:1, :])

    v = xs_ref[...] * dtx
    cb = c_ref[...].astype(BF16)
    bb = b_ref[...].astype(BF16)
    scores = lax.dot_general(cb, bb, (((1,), (1,)), ((), ())), preferred_element_type=F32)
    h_old = h_ref[...]
    y_inter = jnp.exp(acx) * jnp.dot(cb, h_old.astype(BF16), preferred_element_type=F32)
    vw = (v * jnp.exp(tot - acx)).astype(BF16)
    local = lax.dot_general(bb, vw, (((0,), (0,)), ((), ())), preferred_element_type=F32)
    h_ref[...] = h_old * jnp.exp(tot) + local

    ac_t = ac_r.T
    lane = lax.broadcasted_iota(jnp.int32, (Q, LANES), 1)
    heads_per_slab = LANES // p_dim
    for slab in range(gw // LANES):
        v_slab = v[:, slab * LANES:(slab + 1) * LANES]
        acc = y_inter[:, slab * LANES:(slab + 1) * LANES]
        for hh in range(heads_per_slab):
            r = slab * heads_per_slab + hh
            seg = ac_r[:, r:r + 1] - ac_t[r:r + 1, :]
            decay = jnp.where(tri_b, jnp.exp(seg), 0.0)
            vm = jnp.where((lane >= hh * p_dim) & (lane < (hh + 1) * p_dim), v_slab, 0.0)
            acc = acc + _bdot(scores * decay, vm)
        y_ref[:, slab * LANES:(slab + 1) * LANES] = acc

    @pl.when(last & is_ctx)
    def _():
        st_ref[...] = h_ref[...].T


def _ssd_scan(xbc, dtr, dt_bias, a_log, h0, layer):
    gw = SSD_HEADS_PER_GROUP * SSD_HEAD_DIM
    n = D_STATE
    nb_b = SSD_INNER // n
    blocks = [((Q, gw), F32), ((Q, n), F32), ((Q, n), F32), ((Q, LANES), F32), ((gw, n), F32),
              ((Q, gw), F32), ((gw, n), F32)]
    return pl.pallas_call(
        _ssd_scan_kernel,
        out_shape=[jax.ShapeDtypeStruct((2, T_ALL, SSD_INNER), F32),
                   jax.ShapeDtypeStruct((BATCH, 2, SSD_INNER, n), F32)],
        grid=(SSD_GROUPS, 2, NCH),
        in_specs=[pl.BlockSpec((Q, gw), lambda g, d, c: (_chunk_of(d, c), g)),
                  pl.BlockSpec((Q, n), lambda g, d, c: (_chunk_of(d, c), nb_b + g)),
                  pl.BlockSpec((Q, n), lambda g, d, c: (_chunk_of(d, c), nb_b + SSD_GROUPS + g)),
                  pl.BlockSpec((Q, LANES), lambda g, d, c: (_chunk_of(d, c), 0)),
                  pl.BlockSpec((None, 1, LANES), lambda g, d, c: (layer, 0, 0)),
                  pl.BlockSpec((None, 1, LANES), lambda g, d, c: (layer, 0, 0)),
                  pl.BlockSpec((None, None, None, gw, n),
                               lambda g, d, c: (_dec_seq_of(_chunk_of(d, c)), layer, d, g, 0))],
        out_specs=[pl.BlockSpec((None, Q, gw), lambda g, d, c: (d, _chunk_of(d, c), g)),
                   pl.BlockSpec((None, None, gw, n), lambda g, d, c: (_ctx_seq_of(_chunk_of(d, c)), d, g, 0))],
        scratch_shapes=[pltpu.VMEM((n, gw), F32)],
        compiler_params=_params(("arbitrary", "arbitrary", "arbitrary"), blocks, [((n, gw), F32)]),
        name="ssd_scan",
    )(xbc, xbc, xbc, dtr, dt_bias, a_log, h0)


def _ret_scan_kernel(lg_ref, q_ref, k_ref, v_ref, h0_ref, y_ref, st_ref, s_ref, *, layer):
    h = pl.program_id(0)
    d = pl.program_id(1)
    ce = _chunk_of(d, pl.program_id(2))
    is_ctx, first, last = _scan_flags(d, ce)
    fwd = d == 0
    dv = v_ref.shape[1]
    lg = lg_ref[(layer * 2 + d) * RET_HEADS + h]

    @pl.when(first & is_ctx)
    def _():
        s_ref[...] = jnp.zeros_like(s_ref)

    @pl.when(first & jnp.logical_not(is_ctx))
    def _():
        s_ref[...] = h0_ref[...]

    ii = lax.broadcasted_iota(jnp.int32, (Q, Q), 0)
    jj = lax.broadcasted_iota(jnp.int32, (Q, Q), 1)
    dist = jnp.where(fwd, ii - jj, jj - ii)
    decay = jnp.where(dist >= 0, jnp.exp(dist.astype(F32) * lg), 0.0)
    row = lax.broadcasted_iota(jnp.int32, (Q, dv), 0)
    n_in = jnp.where(fwd, row + 1, Q - row).astype(F32)
    n_out = jnp.where(fwd, Q - 1 - row, row).astype(F32)
    q = q_ref[...]
    k = k_ref[...]
    v = v_ref[...]
    scores = lax.dot_general(q, k, (((1,), (1,)), ((), ())), preferred_element_type=F32)
    s_old = s_ref[...]
    y_inter = lax.dot_general(q, s_old.astype(BF16), (((1,), (1,)), ((), ())), preferred_element_type=F32)
    y_ref[...] = _bdot(scores * decay, v) + jnp.exp(n_in * lg) * y_inter
    vw = (v * jnp.exp(n_out * lg)).astype(BF16)
    local = lax.dot_general(vw, k, (((0,), (0,)), ((), ())), preferred_element_type=F32)
    s_ref[...] = s_old * jnp.exp(jnp.full(s_old.shape, float(Q), F32) * lg) + local

    @pl.when(last & is_ctx)
    def _():
        st_ref[...] = s_ref[...]


def _ret_scan(qk, qkvg, v_col0, log_gamma, h0, layer):
    dk, dv = RET_DK, RET_DV
    vb0 = v_col0 // dv
    blocks = [((Q, dk), BF16), ((Q, dk), BF16), ((Q, dv), F32), ((dv, dk), F32), ((Q, dv), F32), ((dv, dk), F32)]
    return pl.pallas_call(
        functools.partial(_ret_scan_kernel, layer=layer),
        out_shape=[jax.ShapeDtypeStruct((2, T_ALL, RET_INNER), F32),
                   jax.ShapeDtypeStruct((BATCH, 2, RET_HEADS, dv, dk), F32)],
        grid=(RET_HEADS, 2, NCH),
        in_specs=[pl.BlockSpec(memory_space=pltpu.SMEM),
                  pl.BlockSpec((Q, dk), lambda h, d, c: (_chunk_of(d, c), h)),
                  pl.BlockSpec((Q, dk), lambda h, d, c: (_chunk_of(d, c), RET_HEADS + h)),
                  pl.BlockSpec((Q, dv), lambda h, d, c: (_chunk_of(d, c), vb0 + h)),
                  pl.BlockSpec((None, None, None, None, dv, dk),
                               lambda h, d, c: (_dec_seq_of(_chunk_of(d, c)), layer, d, h, 0, 0))],
        out_specs=[pl.BlockSpec((None, Q, dv), lambda h, d, c: (d, _chunk_of(d, c), h)),
                   pl.BlockSpec((None, None, None, dv, dk),
                                lambda h, d, c: (_ctx_seq_of(_chunk_of(d, c)), d, h, 0, 0))],
        scratch_shapes=[pltpu.VMEM((dv, dk), F32)],
        compiler_params=_params(("arbitrary", "arbitrary", "arbitrary"), blocks, [((dv, dk), F32)]),
        name="ret_scan",
    )(log_gamma.reshape(-1), qk, qk, qkvg, h0)


def _rope_kernel(x_ref, cos_ref, sin_ref, o_ref):
    tm = x_ref.shape[0]
    is_ctx = pl.program_id(0) * tm >= T_S
    x = x_ref[...]
    half = x.shape[1] // 2
    cos = jnp.where(is_ctx, 1.0, cos_ref[...])
    sin = jnp.where(is_ctx, 0.0, sin_ref[...])
    swapped = jnp.concatenate([pltpu.roll(x[:, :half], half // 2, 1), pltpu.roll(x[:, half:], half // 2, 1)], axis=1)
    scale = jnp.where(pl.program_id(1) >= RET_HEADS, RET_DK ** -0.5, 1.0)
    o_ref[...] = ((x * cos + swapped * sin) * scale).astype(o_ref.dtype)


def _rope_tables():
    half = RET_DK // 2
    n = half // 2
    t = np.arange(DEC_SEQ)
    rows = (t // GRID_W).astype(np.float32)
    cols = (t % GRID_W).astype(np.float32)
    inv = (ROPE_BASE ** (-np.arange(n, dtype=np.float32) / n)).astype(np.float32)

    def tab(pos):
        ang = pos[:, None] * inv[None, :]
        c = np.cos(ang)
        s = np.sin(ang)
        return np.concatenate([c, c], axis=1), np.concatenate([-s, s], axis=1)

    cr, sr = tab(rows)
    cc, sc = tab(cols)
    return (jnp.asarray(np.concatenate([cr, cc], axis=1), F32), jnp.asarray(np.concatenate([sr, sc], axis=1), F32))


def _rope(qkvg, cos_t, sin_t):
    assert RET_DK // 2 == LANES
    tm = _pick(math.gcd(T_S, T_C), (512, 256, 128))
    dk = RET_DK
    pos_blocks = DEC_SEQ // tm
    blocks = [((tm, dk), F32)] * 3 + [((tm, dk), BF16)]
    return pl.pallas_call(
        _rope_kernel,
        out_shape=jax.ShapeDtypeStruct((T_ALL, 2 * RET_INNER), BF16),
        grid=(T_ALL // tm, 2 * RET_HEADS),
        in_specs=[pl.BlockSpec((tm, dk), lambda m, j: (m, j)),
                  pl.BlockSpec((tm, dk), lambda m, j: (jnp.where(m * tm < T_S, m % pos_blocks, 0), 0)),
                  pl.BlockSpec((tm, dk), lambda m, j: (jnp.where(m * tm < T_S, m % pos_blocks, 0), 0))],
        out_specs=pl.BlockSpec((tm, dk), lambda m, j: (m, j)),
        compiler_params=_params(("arbitrary", "arbitrary"), blocks),
        name="rope_qk",
    )(qkvg, cos_t, sin_t)


def _ab_post_kernel(ys_ref, xs_ref, z_ref, dsk_ref, snw_ref, yr_ref, g_ref, rnw_ref, o_ref):
    y = ys_ref[0] + ys_ref[1] + xs_ref[...] * dsk_ref[...]
    y = y * _silu(z_ref[...])
    gwid = SSD_INNER // SSD_GROUPS
    for gi in range(SSD_GROUPS):
        seg = y[:, gi * gwid:(gi + 1) * gwid]
        seg = seg * lax.rsqrt(jnp.mean(seg * seg, axis=-1, keepdims=True) + EPS)
        o_ref[:, gi * gwid:(gi + 1) * gwid] = (seg * snw_ref[:, gi * gwid:(gi + 1) * gwid]).astype(o_ref.dtype)
    yr = yr_ref[0] + yr_ref[1]
    gate = _silu(g_ref[...])
    for hi in range(RET_HEADS):
        seg = yr[:, hi * RET_DV:(hi + 1) * RET_DV]
        mu = jnp.mean(seg, axis=-1, keepdims=True)
        cen = seg - mu
        var = jnp.mean(cen * cen, axis=-1, keepdims=True)
        out = cen * lax.rsqrt(var + EPS) * rnw_ref[:, hi * RET_DV:(hi + 1) * RET_DV]
        out = gate[:, hi * RET_DV:(hi + 1) * RET_DV] * out
        o_ref[:, SSD_INNER + hi * RET_DV:SSD_INNER + (hi + 1) * RET_DV] = out.astype(o_ref.dtype)


def _ab_post(ys, xbc, zx, d_skip_x, ssd_norm_w, yr, qkvg, g_col0, ret_norm_w, layer):
    tm = _pick(T_ALL, (256, 128))
    si, ri = SSD_INNER, RET_INNER
    gb0 = g_col0 // ri
    blocks = [((2, tm, si), F32), ((tm, si), F32), ((tm, si), F32), ((2, tm, ri), F32), ((tm, ri), F32),
              ((tm, si + ri), BF16)]
    return pl.pallas_call(
        _ab_post_kernel,
        out_shape=jax.ShapeDtypeStruct((T_ALL, si + ri), BF16),
        grid=(T_ALL // tm,),
        in_specs=[pl.BlockSpec((2, tm, si), lambda m: (0, m, 0)),
                  pl.BlockSpec((tm, si), lambda m: (m, 0)),
                  pl.BlockSpec((tm, si), lambda m: (m, 0)),
                  pl.BlockSpec((None, 1, si), lambda m: (layer, 0, 0)),
                  pl.BlockSpec((None, 1, si), lambda m: (layer, 0, 0)),
                  pl.BlockSpec((2, tm, ri), lambda m: (0, m, 0)),
                  pl.BlockSpec((tm, ri), lambda m: (m, gb0)),
                  pl.BlockSpec((None, 1, ri), lambda m: (layer, 0, 0))],
        out_specs=pl.BlockSpec((tm, si + ri), lambda m: (m, 0)),
        compiler_params=_params(("arbitrary",), blocks),
        name="ab_post",
    )(ys, xbc, zx, d_skip_x, ssd_norm_w, yr, qkvg, ret_norm_w)


def _hy_filter_kernel(f_ref, w1_ref, b1_ref, w2_ref, b2_ref, w3_ref, dl_ref, o_ref):
    seq_len = f_ref.shape[0]
    h = jnp.sin(HY_SIN_FREQ * (jnp.dot(f_ref[...], w1_ref[...], preferred_element_type=F32, precision=HIGHEST)
                               + b1_ref[...]))
    h = jnp.sin(HY_SIN_FREQ * (jnp.dot(h, w2_ref[...], preferred_element_type=F32, precision=HIGHEST)
                               + b2_ref[...]))
    h = jnp.dot(h, w3_ref[...], preferred_element_type=F32, precision=HIGHEST)
    t = lax.broadcasted_iota(jnp.int32, h.shape, 0).astype(F32) / seq_len
    h = h * jnp.exp(-t * dl_ref[...])
    o_ref[...] = h / (jnp.sum(jnp.abs(h), axis=0, keepdims=True) + EPS)


def _hy_feats(seq_len):
    t = jnp.arange(seq_len, dtype=F32) / seq_len
    bands = jnp.linspace(1e-4, HY_BANDS - 1, HY_BANDS, dtype=F32)
    ang = 2 * math.pi * t[:, None] * bands[None, :]
    feats = jnp.concatenate([t[:, None], jnp.cos(ang), jnp.sin(ang)], axis=-1)
    return jnp.pad(feats, ((0, 0), (0, LANES - HY_EMB)))


def _hy_deltas():
    deltas = jnp.abs(jnp.linspace(math.log(HY_TARGET) / HY_SLOW_DECAY, math.log(HY_TARGET) / HY_FAST_DECAY,
                                  HY_WIDTH, dtype=F32))
    return jnp.tile(deltas, 2 * HY_ORDER)[None, :]


def _hy_filters(feats, w1p, b1, w2, b2, w3, deltas, layer):
    seq_len = feats.shape[0]
    n = w3.shape[2]
    tn = _pick(n, (256, 128))
    hid = HY_HIDDEN
    blocks = [((seq_len, LANES), F32), ((hid, tn), F32), ((seq_len, tn), F32)]
    return pl.pallas_call(
        _hy_filter_kernel,
        out_shape=jax.ShapeDtypeStruct((seq_len, n), F32),
        grid=(n // tn,),
        in_specs=[pl.BlockSpec((seq_len, LANES), lambda j: (0, 0)),
                  pl.BlockSpec((None, LANES, hid), lambda j: (layer, 0, 0)),
                  pl.BlockSpec((None, 1, hid), lambda j: (layer, 0, 0)),
                  pl.BlockSpec((None, hid, hid), lambda j: (layer, 0, 0)),
                  pl.BlockSpec((None, 1, hid), lambda j: (layer, 0, 0)),
                  pl.BlockSpec((None, hid, tn), lambda j: (layer, 0, j)),
                  pl.BlockSpec((1, tn), lambda j: (0, j))],
        out_specs=pl.BlockSpec((seq_len, tn), lambda j: (0, j)),
        compiler_params=pltpu.CompilerParams(dimension_semantics=("arbitrary",), vmem_limit_bytes=VMEM_CAP),
        name="hy_filters",
    )(feats, w1p, b1, w2, b2, w3, deltas)


def _fft4_shape(seq_len):
    n = 2 * seq_len
    n2 = 128 if n >= 4096 else 64
    return n // n2, n2


def _fft4_tables(seq_len):
    n1, n2 = _fft4_shape(seq_len)
    n = n1 * n2
    hn1 = n1 // 2
    l2 = np.arange(n2)[:, None, None]
    k1 = np.arange(n1)[None, :, None]
    l1 = np.arange(hn1)[None, None, :]
    th = 2 * np.pi * ((k1 * (n2 * l1 + l2)) % n) / n
    ec, es = np.cos(th), np.sin(th)
    e = np.concatenate([np.concatenate([ec, es], axis=2), np.concatenate([-es, ec], axis=2)], axis=1)
    ei = np.transpose(e, (0, 2, 1)) / n
    a = 2 * np.pi * ((np.arange(n2)[:, None] * np.arange(n2)[None, :]) % n2) / n2
    wc, ws = np.cos(a), np.sin(a)
    w = np.concatenate([np.concatenate([wc, ws], axis=1), np.concatenate([-ws, wc], axis=1)], axis=0)
    as_bf = lambda x: jnp.asarray(x.astype(np.float32)).astype(BF16)
    return as_bf(e), as_bf(ei), as_bf(w), as_bf(w.T)


def _fft4_stage1(load_a, load_b, e_ref, a_ref, n1, n2):
    def body(l2, carry):
        x = jnp.concatenate([load_a(l2), load_b(l2)], axis=0).astype(BF16)
        a_ref[pl.ds(pl.multiple_of(l2 * 2 * n1, 2 * n1), 2 * n1), :] = jnp.dot(
            e_ref[l2], x, preferred_element_type=F32)
        return carry
    lax.fori_loop(0, n2, body, 0)


def _fft4_conv_kernel(z_ref, g_ref, bias_ref, h_ref, e_ref, ei_ref, w_ref, wi_ref, o_ref, a_ref, *, n1, n2):
    hn1 = n1 // 2
    seq_len = hn1 * n2
    _fft4_stage1(lambda l2: z_ref[pl.ds(l2, hn1, stride=n2), :],
                 lambda l2: z_ref[pl.ds(seq_len + l2, hn1, stride=n2), :], e_ref, a_ref, n1, n2)

    def stage2(k1, carry):
        ar = a_ref[pl.ds(k1, n2, stride=2 * n1), :]
        ai = a_ref[pl.ds(n1 + k1, n2, stride=2 * n1), :]
        z = jnp.dot(w_ref[...], jnp.concatenate([ar, ai], axis=0).astype(BF16), preferred_element_type=F32)
        hs = h_ref[pl.ds(pl.multiple_of(k1 * 2 * n2, 2 * n2), 2 * n2), :]
        zr, zi, hr, hi = z[:n2], z[n2:], hs[:n2], hs[n2:]
        y = jnp.concatenate([zr * hr - zi * hi, zr * hi + zi * hr], axis=0).astype(BF16)
        b = jnp.dot(wi_ref[...], y, preferred_element_type=F32)
        a_ref[pl.ds(k1, n2, stride=2 * n1), :] = b[:n2]
        a_ref[pl.ds(n1 + k1, n2, stride=2 * n1), :] = b[n2:]
        return carry
    lax.fori_loop(0, n1, stage2, 0)

    bias = bias_ref[...]

    def stage3(l2, carry):
        bl = a_ref[pl.ds(pl.multiple_of(l2 * 2 * n1, 2 * n1), 2 * n1), :].astype(BF16)
        y = jnp.dot(ei_ref[l2], bl, preferred_element_type=F32)
        for half, base in ((0, 0), (1, seq_len)):
            idx = pl.ds(base + l2, hn1, stride=n2)
            zz = z_ref[idx, :]
            o_ref[idx, :] = g_ref[idx, :] * (y[half * hn1:(half + 1) * hn1] + bias * zz)
        return carry
    lax.fori_loop(0, n2, stage3, 0)


def _fft4_spec_kernel(hf_ref, hb_ref, e_ref, w_ref, o_ref, au_ref, av_ref, *, n1, n2):
    hn1 = n1 // 2
    tc = hf_ref.shape[1]
    rows = lax.broadcasted_iota(jnp.int32, (hn1, tc), 0)

    def hb_rows(l2):
        return jnp.where((rows == 0) & (l2 == 0), 0.0, hb_ref[pl.ds(l2, hn1, stride=n2), :])

    zeros = jnp.zeros((hn1, tc), F32)
    _fft4_stage1(lambda l2: hf_ref[pl.ds(l2, hn1, stride=n2), :] + hb_rows(l2), lambda l2: zeros,
                 e_ref, au_ref, n1, n2)
    _fft4_stage1(lambda l2: hf_ref[pl.ds(l2, hn1, stride=n2), :] - hb_rows(l2), lambda l2: zeros,
                 e_ref, av_ref, n1, n2)

    def stage2(k1, carry):
        outs = []
        for a_ref, lo in ((au_ref, 0), (av_ref, n2)):
            ar = a_ref[pl.ds(k1, n2, stride=2 * n1), :]
            ai = a_ref[pl.ds(n1 + k1, n2, stride=2 * n1), :]
            z = jnp.dot(w_ref[...], jnp.concatenate([ar, ai], axis=0).astype(BF16), preferred_element_type=F32)
            outs.append(z[lo:lo + n2])
        o_ref[pl.ds(pl.multiple_of(k1 * 2 * n2, 2 * n2), 2 * n2), :] = jnp.concatenate(outs, axis=0)
        return carry
    lax.fori_loop(0, n1, stage2, 0)


def _fft4_spec(h_all, order, tables):
    seq_len = h_all.shape[0]
    n1, n2 = _fft4_shape(seq_len)
    e, _, w, _ = tables
    c = HY_WIDTH
    tc = LANES
    cb = c // tc
    blocks = [((seq_len, tc), F32)] * 2 + [(e.shape, BF16), (w.shape, BF16), ((2 * n1 * n2, tc), F32)]
    scratch = [((2 * n1 * n2, tc), F32)] * 2
    return pl.pallas_call(
        functools.partial(_fft4_spec_kernel, n1=n1, n2=n2),
        out_shape=jax.ShapeDtypeStruct((2 * n1 * n2, c), F32),
        grid=(cb,),
        in_specs=[pl.BlockSpec((seq_len, tc), lambda j: (0, order * cb + j)),
                  pl.BlockSpec((seq_len, tc), lambda j: (0, (HY_ORDER + order) * cb + j)),
                  _resident(e.shape, lambda j: (0, 0, 0)),
                  _resident(w.shape, lambda j: (0, 0))],
        out_specs=pl.BlockSpec((2 * n1 * n2, tc), lambda j: (0, j)),
        scratch_shapes=[pltpu.VMEM(s, dt) for s, dt in scratch],
        compiler_params=_params(("arbitrary",), blocks, scratch),
        name="hy_spec_latent",
    )(h_all, h_all, e, w)


def _fft4_conv(z, z_col0, gate, gate_col0, bias, order, layer, spec, tables):
    assert DEC_BATCH == 2
    seq_len = DEC_SEQ
    n1, n2 = _fft4_shape(seq_len)
    e, ei, w, wi = tables
    c = HY_WIDTH
    tc = LANES
    zb0, gb0 = z_col0 // tc, gate_col0 // tc
    rows = 2 * seq_len
    blocks = [((rows, tc), F32)] * 3
    scratch = [((2 * n1 * n2, tc), F32)] * 2 + [(e.shape, BF16), (ei.shape, BF16)]
    return pl.pallas_call(
        functools.partial(_fft4_conv_kernel, n1=n1, n2=n2),
        out_shape=jax.ShapeDtypeStruct((T_ALL, c), F32),
        grid=(c // tc,),
        in_specs=[pl.BlockSpec((rows, tc), lambda j: (0, zb0 + j)),
                  pl.BlockSpec((rows, tc), lambda j: (0, gb0 + j)),
                  pl.BlockSpec((None, None, 1, tc), lambda j: (layer, order, 0, j)),
                  _resident((2 * n1 * n2, tc), lambda j: (0, j)),
                  _resident(e.shape, lambda j: (0, 0, 0)),
                  _resident(ei.shape, lambda j: (0, 0, 0)),
                  _resident(w.shape, lambda j: (0, 0)),
                  _resident(wi.shape, lambda j: (0, 0))],
        out_specs=pl.BlockSpec((rows, tc), lambda j: (0, j)),
        scratch_shapes=[pltpu.VMEM((2 * n1 * n2, tc), F32)],
        compiler_params=_params(("arbitrary",), blocks, scratch),
        name="hy_conv_latent",
    )(z, gate, bias, spec, e, ei, w, wi)


def _dft_tables(seq_len):
    n = 2 * seq_len
    a = 2 * np.pi * ((np.arange(n)[:, None] * np.arange(seq_len)[None, :]) % n) / n
    fc, fs = np.cos(a), np.sin(a)
    f = np.concatenate([np.concatenate([fc, fs], axis=1), np.concatenate([-fs, fc], axis=1)], axis=0)
    as_bf = lambda x: jnp.asarray(x.astype(np.float32)).astype(BF16)
    return as_bf(f), as_bf(f.T / n), as_bf(fc), as_bf(fs)


def _dft_spec_kernel(hf_ref, hb_ref, fc_ref, fs_ref, o_ref):
    n = fc_ref.shape[0]
    hf = hf_ref[...]
    rows = lax.broadcasted_iota(jnp.int32, hf.shape, 0)
    hb = jnp.where(rows == 0, 0.0, hb_ref[...])
    o_ref[0:n, :] = jnp.dot(fc_ref[...], (hf + hb).astype(BF16), preferred_element_type=F32)
    o_ref[n:2 * n, :] = -jnp.dot(fs_ref[...], (hf - hb).astype(BF16), preferred_element_type=F32)


def _dft_spec(h_all, order, tables):
    seq_len = h_all.shape[0]
    _, _, fc, fs = tables
    n = 2 * seq_len
    c = HY_WIDTH
    tc = _pick(c, (512, 256, 128))
    cb = c // tc
    blocks = [((seq_len, tc), F32)] * 2 + [(fc.shape, BF16)] * 2 + [((2 * n, tc), F32)]
    return pl.pallas_call(
        _dft_spec_kernel,
        out_shape=jax.ShapeDtypeStruct((2 * n, c), F32),
        grid=(cb,),
        in_specs=[pl.BlockSpec((seq_len, tc), lambda j: (0, order * cb + j)),
                  pl.BlockSpec((seq_len, tc), lambda j: (0, (HY_ORDER + order) * cb + j)),
                  pl.BlockSpec(fc.shape, lambda j: (0, 0)),
                  pl.BlockSpec(fs.shape, lambda j: (0, 0))],
        out_specs=pl.BlockSpec((2 * n, tc), lambda j: (0, j)),
        compiler_params=_params(("arbitrary",), blocks),
        name="hy_spec_context",
    )(h_all, h_all, fc, fs)


def _dft_conv_kernel(z_ref, g_ref, bias_ref, h_ref, f_ref, fi_ref, buf_ref, o_ref):
    del buf_ref
    n = f_ref.shape[0] // 2
    x = z_ref[...]
    z = jnp.dot(f_ref[...], x.astype(BF16), preferred_element_type=F32)
    zr, zi, hr, hi = z[:n], z[n:], h_ref[0:n, :], h_ref[n:2 * n, :]
    y = jnp.concatenate([zr * hr - zi * hi, zr * hi + zi * hr], axis=0).astype(BF16)
    y = jnp.dot(fi_ref[...], y, preferred_element_type=F32)
    o_ref[...] = g_ref[...] * (y + bias_ref[...] * x)


def _dft_conv(z, z_col0, gate, gate_col0, bias, order, layer, spec, tables, buf):
    assert BATCH % 2 == 0 and T_S % (2 * SEQ) == 0
    seq_len = SEQ
    f, fi, _, _ = tables
    n = 2 * seq_len
    c = HY_WIDTH
    tc = _pick(math.gcd(c, math.gcd(z_col0, gate_col0) or c), (512, 256, 128))
    zb0, gb0 = z_col0 // tc, gate_col0 // tc
    rows = 2 * seq_len
    rb0 = T_S // rows
    blocks = [((rows, tc), F32)] * 3 + [((2 * n, tc), F32), (f.shape, BF16), (fi.shape, BF16)]
    return pl.pallas_call(
        _dft_conv_kernel,
        out_shape=jax.ShapeDtypeStruct(buf.shape, buf.dtype),
        grid=(c // tc, BATCH // 2),
        in_specs=[pl.BlockSpec((rows, tc), lambda j, p: (rb0 + p, zb0 + j)),
                  pl.BlockSpec((rows, tc), lambda j, p: (rb0 + p, gb0 + j)),
                  pl.BlockSpec((None, None, 1, tc), lambda j, p: (layer, order, 0, j)),
                  pl.BlockSpec((2 * n, tc), lambda j, p: (0, j)),
                  pl.BlockSpec(f.shape, lambda j, p: (0, 0)),
                  pl.BlockSpec(fi.shape, lambda j, p: (0, 0)),
                  pl.BlockSpec(memory_space=pl.ANY)],
        out_specs=pl.BlockSpec((rows, tc), lambda j, p: (rb0 + p, j)),
        input_output_aliases={6: 0},
        compiler_params=_params(("arbitrary", "arbitrary"), blocks),
        name="hy_conv_context",
    )(z, gate, bias, spec, f, fi, buf)


def _ab_layer(x, h, mods, layer, i, p, state_ssd, state_ret, rope_tabs):
    o1 = SSD_INNER
    o2 = o1 + SSD_CONV_DIM
    o3 = o2 + 2 * SSD_HEADS
    zx = _matmul(h, p["ab_w_in"], i, 0, o2, "ab_in_zx")
    dtr = _matmul(h, p["ab_w_dt"], i, 0, LANES, "ab_in_dt")
    qkvg = _matmul(h, p["ab_w_qkvg"], i, 0, 4 * RET_INNER, "ab_in_qkvg")
    xbc = _dwconv(zx, o1, SSD_CONV_DIM, p["ab_conv_w"], p["ab_conv_b"], i, "ssd_conv", act="silu")
    ys, st_s = _ssd_scan(xbc, dtr, p["dt_bias_p"], p["a_log_p"], state_ssd, i)
    qk = _rope(qkvg, *rope_tabs)
    yr, st_r = _ret_scan(qk, qkvg, 2 * RET_INNER, p["ret_log_gamma"], state_ret, i)
    cat = _ab_post(ys, xbc, zx, p["d_skip_x"], p["ssd_norm_w3"], yr, qkvg, 3 * RET_INNER, p["ret_norm_w3"], i)
    x = _matmul(cat, p["ab_w_out"], i, 0, D_MODEL, "ab_out", res=x, mods=mods, gate_idx=(layer, 2))
    return x, st_s, st_r


def _hy_layer(x, h, mods, layer, i, p, hy_consts):
    c = HY_WIDTH
    proj = _matmul(h, p["hy_w_in"], i, 0, (HY_ORDER + 1) * c, "hy_in")
    pc = _dwconv(proj, 0, (HY_ORDER + 1) * c, p["hy_short_w"], p["hy_short_b"], i, "hy_short_conv")
    feats_s, feats_c, deltas, tab4, tabd = hy_consts
    filt_s = _hy_filters(feats_s, p["hy_f_w1p"], p["hy_f_b1"], p["hy_f_w2"], p["hy_f_b2"], p["hy_f_w3"], deltas, i)
    filt_c = _hy_filters(feats_c, p["hy_f_w1p"], p["hy_f_b1"], p["hy_f_w2"], p["hy_f_b2"], p["hy_f_w3"], deltas, i)
    z, z_col0 = pc, HY_ORDER * c
    for o in range(HY_ORDER):
        spec_s = _fft4_spec(filt_s, o, tab4)
        spec_c = _dft_spec(filt_c, o, tabd)
        buf = _fft4_conv(z, z_col0, pc, o * c, p["hy_bias4"], o, i, spec_s, tab4)
        z = _dft_conv(z, z_col0, pc, o * c, p["hy_bias4"], o, i, spec_c, tabd, buf)
        z_col0 = 0
    return _matmul(z, p["hy_w_out"], i, 0, D_MODEL, "hy_out", res=x, mods=mods, gate_idx=(layer, 2))


def _ffn(x, h, mods, layer, p):
    gate = _matmul(h, p["ffn_w_gate"], layer, 0, D_FF, "ffn_gate")
    up = _matmul(h, p["ffn_w_up"], layer, 0, D_FF, "ffn_up")
    act = _dwconv(gate, 0, D_FF, p["ffn_conv_w"], p["ffn_conv_b"], layer, "ffn_glu", up=up, out_dtype=BF16)
    return _matmul(act, p["ffn_w_down"], layer, 0, D_MODEL, "ffn_down", res=x, mods=mods, gate_idx=(layer, 5))


def kernel(x_prompt, x_sample, c, state_ssd, state_ret, c_ctx, ada_w, ada_b, norm1_w, norm2_w, ab_w_in, ab_conv_w, ab_conv_b, ssd_dt_bias, ssd_a_log, ssd_d, ssd_norm_w, ret_log_gamma, ret_norm_w, ab_w_out, hy_w_in, hy_short_w, hy_short_b, hy_f_w1, hy_f_b1, hy_f_w2, hy_f_b2, hy_f_w3, hy_bias, hy_w_out, ffn_w_gate, ffn_w_up, ffn_conv_w, ffn_conv_b, ffn_w_down, final_norm_w):
    d = D_MODEL
    la, lc = N_AB_LAYERS, N_C_LAYERS
    o2 = SSD_INNER + SSD_CONV_DIM
    o3 = o2 + 2 * SSD_HEADS
    lane_pad = LANES - 2 * SSD_HEADS
    p = {
        "ab_w_in": ab_w_in,
        "ab_w_dt": jnp.pad(ab_w_in[:, :, o2:o3], ((0, 0), (0, 0), (0, lane_pad))),
        "ab_w_qkvg": ab_w_in[:, :, o3:],
        "ab_conv_w": ab_conv_w, "ab_conv_b": ab_conv_b,
        "dt_bias_p": jnp.pad(ssd_dt_bias.reshape(la, 1, 2 * SSD_HEADS), ((0, 0), (0, 0), (0, lane_pad))),
        "a_log_p": jnp.pad(ssd_a_log.reshape(la, 1, 2 * SSD_HEADS), ((0, 0), (0, 0), (0, lane_pad))),
        "d_skip_x": jnp.repeat(ssd_d, SSD_HEAD_DIM, axis=1).reshape(la, 1, SSD_INNER),
        "ssd_norm_w3": ssd_norm_w.reshape(la, 1, SSD_INNER),
        "ret_log_gamma": ret_log_gamma,
        "ret_norm_w3": ret_norm_w.reshape(la, 1, RET_INNER),
        "ab_w_out": ab_w_out,
        "hy_w_in": hy_w_in, "hy_short_w": hy_short_w, "hy_short_b": hy_short_b,
        "hy_f_w1p": jnp.pad(hy_f_w1, ((0, 0), (0, LANES - HY_EMB), (0, 0))),
        "hy_f_b1": hy_f_b1.reshape(lc, 1, HY_HIDDEN), "hy_f_w2": hy_f_w2,
        "hy_f_b2": hy_f_b2.reshape(lc, 1, HY_HIDDEN), "hy_f_w3": hy_f_w3,
        "hy_bias4": hy_bias.reshape(lc, HY_ORDER, 1, HY_WIDTH), "hy_w_out": hy_w_out,
        "ffn_w_gate": ffn_w_gate, "ffn_w_up": ffn_w_up, "ffn_conv_w": ffn_conv_w, "ffn_conv_b": ffn_conv_b,
        "ffn_w_down": ffn_w_down,
    }
    x = jnp.concatenate([x_sample.reshape(T_S, d), x_prompt.reshape(T_C, d)], axis=0)
    cond = jnp.concatenate([c_ctx[None, :], c, jnp.zeros((N_COND - 1 - DEC_BATCH, d), F32)], axis=0)
    mods = _ada(cond.T, ada_w, ada_b).reshape(DEPTH * N_COND * 6, 1, d)
    rope_tabs = _rope_tables()
    hy_consts = (_hy_feats(DEC_SEQ), _hy_feats(SEQ), _hy_deltas(), _fft4_tables(DEC_SEQ), _dft_tables(SEQ))
    h0_ssd = state_ssd.reshape(DEC_BATCH, la, 2, SSD_INNER, D_STATE)
    fin_ssd, fin_ret = [], []
    for layer in range(DEPTH):
        i = layer // 2
        h = _norm_mod(x, norm1_w, mods, layer, 0, 1)
        if layer % 2 == 0:
            x, st_s, st_r = _ab_layer(x, h, mods, layer, i, p, h0_ssd, state_ret, rope_tabs)
            fin_ssd.append(st_s.reshape(BATCH, 2, SSD_HEADS, SSD_HEAD_DIM, D_STATE))
            fin_ret.append(st_r)
        else:
            x = _hy_layer(x, h, mods, layer, i, p, hy_consts)
        h = _norm_mod(x, norm2_w, mods, layer, 3, 4)
        x = _ffn(x, h, mods, layer, p)
    y = _final_norm(x, final_norm_w)
    y_sample = y[:T_S].reshape(DEC_BATCH, DEC_SEQ, d)
    y_prompt = y[T_S:].reshape(BATCH, SEQ, d)
    return (y_prompt, y_sample, jnp.stack(fin_ssd, axis=1), jnp.stack(fin_ret, axis=1))
```
